```python
import functools
import jax, jax.numpy as jnp
from jax import lax
import numpy as np

D_MODEL = 1024
BATCH = 2
SEQ = 8192
DEPTH = 2
DEC_BATCH = 128
DEC_SEQ = 4
PAST_LEN = 2048
PAGE_SIZE = 128

BRANCH_WIDTH = D_MODEL // 2
N_BRANCHES = 3
HEAD_DIM = 64
SWA_HEADS = BRANCH_WIDTH // HEAD_DIM
DILATED_PATTERNS = ((128, 1), (512, 4), (2048, 16))
MAX_WINDOW = 2048
Q_BLOCK = 128
ROPE_THETA = 10000.0
GMLP_CHUNK = 128
GMLP_GROUPS = 4
RWKV_HEADS = BRANCH_WIDTH // HEAD_DIM
RWKV_DECAY_LORA = 64
RWKV_A_LORA = 64
RWKV_V_LORA = 32
RWKV_GATE_LORA = 160
RWKV_COLS = 3 * BRANCH_WIDTH + RWKV_DECAY_LORA + RWKV_A_LORA + RWKV_GATE_LORA
RWKV_SPLITS = (BRANCH_WIDTH, 2 * BRANCH_WIDTH, 3 * BRANCH_WIDTH,
               3 * BRANCH_WIDTH + RWKV_DECAY_LORA,
               3 * BRANCH_WIDTH + RWKV_DECAY_LORA + RWKV_A_LORA)
IN_SPLITS = (3 * BRANCH_WIDTH, 5 * BRANCH_WIDTH, 5 * BRANCH_WIDTH + RWKV_COLS)
N_IN = 5 * BRANCH_WIDTH + RWKV_COLS + N_BRANCHES * D_MODEL
FFN_DIM = ((8 * D_MODEL // 3 + 127) // 128) * 128
N_EXPERTS = 8
TOP_K = 2
PLE_DIM = 256
N_DENSE = (DEPTH + 1) // 2
N_MOE = DEPTH // 2
RMS_EPS = 1e-6
LNX_EPS = 64e-5
NEG_INF = -1e30

kernel_name = 'hybrid_dilated_gmlp_rwkv7_decoder_step'


def rms_norm(x, gain):
    x32 = x.astype(jnp.float32)
    y = x32 * lax.rsqrt(jnp.mean(x32 * x32, axis=-1, keepdims=True) + RMS_EPS)
    return (y * gain.astype(jnp.float32)).astype(x.dtype)


def rotary(x, pos):
    half = x.shape[-1] // 2
    inv = 1.0 / (ROPE_THETA ** (jnp.arange(half, dtype=jnp.float32) / half))
    ang = pos.astype(jnp.float32)[:, None] * inv[None, :]
    cos = jnp.cos(ang)[:, None, :]
    sin = jnp.sin(ang)[:, None, :]
    x32 = x.astype(jnp.float32)
    x1, x2 = x32[..., :half], x32[..., half:]
    return jnp.concatenate([x1 * cos - x2 * sin, x1 * sin + x2 * cos], axis=-1).astype(x.dtype)


def dilated_attention(q, k_src, v_src, q_idx):
    scale = HEAD_DIM ** -0.5
    outs, lses = [], []
    for window, dil in DILATED_PATTERNS:
        offs = jnp.arange(window // dil + 1, dtype=jnp.int32) * dil
        idx = q_idx[:, None] - offs[None, :]
        valid = idx >= 0
        idx = jnp.maximum(idx, 0)
        kg = k_src[:, idx]
        vg = v_src[:, idx]
        s = jnp.einsum('bqhd,bqjhd->bhqj', q, kg, preferred_element_type=jnp.float32) * scale
        s = jnp.where(valid[None, None], s, NEG_INF)
        m = jnp.max(s, axis=-1, keepdims=True)
        e = jnp.exp(s - m)
        den = jnp.sum(e, axis=-1, keepdims=True)
        outs.append(jnp.einsum('bhqj,bqjhd->bqhd', e / den, vg.astype(jnp.float32)))
        lses.append((m + jnp.log(den))[..., 0])
    wts = jax.nn.softmax(jnp.stack(lses, axis=0), axis=0)
    o = jnp.einsum('pbhq,pbqhd->bqhd', wts, jnp.stack(outs, axis=0))
    return o.astype(q.dtype)


def dilated_attention_prompt(q, k, v):
    B, S, H, D = q.shape
    nblk = S // Q_BLOCK
    qb = q.reshape(B, nblk, Q_BLOCK, H, D).transpose(1, 0, 2, 3, 4)
    starts = jnp.arange(nblk, dtype=jnp.int32) * Q_BLOCK

    def one_block(args):
        qblk, s0 = args
        return dilated_attention(qblk, k, v, s0 + jnp.arange(Q_BLOCK, dtype=jnp.int32))

    o = lax.map(one_block, (qb, starts))
    return o.transpose(1, 0, 2, 3, 4).reshape(B, S, H, D)


def chunk_spatial_gate(u, v, w_s, b_s):
    B, L, W = v.shape
    lc = min(L, GMLP_CHUNK)
    vc = v.reshape(B, L // lc, lc, GMLP_GROUPS, W // GMLP_GROUPS)
    mask = jnp.tril(jnp.ones((lc, lc), dtype=bool))
    wm = jnp.where(mask, w_s[:, :lc, :lc], 0)
    f = jnp.einsum('gij,bcjgd->bcigd', wm, vc) + b_s[:, :lc].T[None, None, :, :, None]
    return u * f.reshape(B, L, W)


def rwkv7_time_mix(zc, shift_prev, wkv_prev, lw, v_first, vres):
    B, L, _ = zc.shape
    f32 = jnp.float32
    z_prev = jnp.concatenate([shift_prev[:, None, :].astype(zc.dtype), zc[:, :-1]], axis=1)
    zs = zc + (z_prev - zc) * lw['rwkv_mu']
    r, k, v, wd, ad, gd = jnp.split(zs, RWKV_SPLITS, axis=-1)
    w_log = -jax.nn.softplus(-(lw['rwkv_w0'] + jnp.tanh(wd) @ lw['rwkv_w2'])) - 0.5
    a = jax.nn.sigmoid(lw['rwkv_a0'] + ad @ lw['rwkv_a2'])
    g = jax.nn.sigmoid(gd) @ lw['rwkv_g2']
    if vres is None:
        v_first = v
    else:
        v0, v1, v2 = vres
        v = v + (v_first - v) * jax.nn.sigmoid(v0 + (v @ v1) @ v2)
    hs = lambda t: t.reshape(B, L, RWKV_HEADS, HEAD_DIM)
    kk = hs(k * lw['rwkv_kk']).astype(f32)
    kk = kk / jnp.maximum(jnp.sqrt(jnp.sum(kk * kk, axis=-1, keepdims=True)), 1e-12)
    k = k * (1 + (a - 1) * lw['rwkv_ka'])
    decay = jnp.exp(-jnp.exp(w_log.astype(f32)))
    r_h, k_h, v_h = hs(r).astype(f32), hs(k).astype(f32), hs(v).astype(f32)
    seq = tuple(t.transpose(1, 0, 2, 3) for t in
                (r_h, hs(decay), k_h, v_h, kk, hs(a).astype(f32)))

    def step(S, inp):
        r_t, w_t, k_t, v_t, kk_t, a_t = inp
        sa = jnp.einsum('bhvk,bhk->bhv', S, -kk_t)
        S = (S * w_t[:, :, None, :] + sa[..., None] * (kk_t * a_t)[:, :, None, :]
             + v_t[..., None] * k_t[:, :, None, :])
        return S, jnp.einsum('bhvk,bhk->bhv', S, r_t)

    S_final, y = lax.scan(step, wkv_prev.astype(f32), seq)
    y = y.transpose(1, 0, 2, 3)
    mu = jnp.mean(y, axis=-1, keepdims=True)
    var = jnp.mean(jnp.square(y - mu), axis=-1, keepdims=True)
    yn = ((y - mu) * lax.rsqrt(var + LNX_EPS)).reshape(B, L, BRANCH_WIDTH)
    yn = yn * lw['rwkv_lnx_w'].astype(f32) + lw['rwkv_lnx_b'].astype(f32)
    bonus = jnp.sum(r_h * k_h * lw['rwkv_rk'].astype(f32), axis=-1, keepdims=True) * v_h
    out = ((yn + bonus.reshape(B, L, BRANCH_WIDTH)) * g.astype(f32)).astype(zc.dtype)
    return out, v_first, zc[:, -1], S_final.astype(zc.dtype)


def swiglu(h, w_gate, w_up, w_down):
    return (jax.nn.silu(h @ w_gate) * (h @ w_up)) @ w_down


def moe_swiglu(h, router, w_gate, w_up, w_down):
    shp = h.shape
    t = h.reshape(-1, shp[-1])
    logits = (t @ router).astype(jnp.float32)
    top_val, top_idx = lax.top_k(logits, TOP_K)
    probs = jax.nn.softmax(top_val, axis=-1)
    combine = jnp.einsum('tk,tke->te', probs, jax.nn.one_hot(top_idx, N_EXPERTS, dtype=jnp.float32))
    y = jnp.zeros_like(t)
    for e in range(N_EXPERTS):
        y = y + combine[:, e:e + 1].astype(t.dtype) * swiglu(t, w_gate[e], w_up[e], w_down[e])
    return y.reshape(shp)


def decoder_layer(x, pe, pos, lw, ffn, vres, v_first, swa_cache, shift_prev, wkv_prev):
    B, L, _ = x.shape
    xn = rms_norm(x, lw['norm_mix'])
    z = xn @ lw['w_in']
    za, zb, zc, zg = jnp.split(z, IN_SPLITS, axis=-1)
    q, k, v = (t.reshape(B, L, SWA_HEADS, HEAD_DIM) for t in jnp.split(za, 3, axis=-1))
    q = rotary(rms_norm(q, lw['q_norm']), pos)
    k = rotary(rms_norm(k, lw['k_norm']), pos)
    if swa_cache is None:
        o_a = dilated_attention_prompt(q, k, v)
        keep = min(MAX_WINDOW, L)
        k_rows, v_rows = k[:, L - keep:], v[:, L - keep:]
    else:
        ck, cv = swa_cache
        o_a = dilated_attention(q, jnp.concatenate([ck.astype(k.dtype), k], axis=1),
                                jnp.concatenate([cv.astype(v.dtype), v], axis=1),
                                ck.shape[1] + jnp.arange(L, dtype=jnp.int32))
        k_rows, v_rows = k, v
    u, vb = jnp.split(jax.nn.gelu(zb), 2, axis=-1)
    vb = rms_norm(vb, lw['gmlp_norm'])
    o_b = chunk_spatial_gate(u, vb, lw['gmlp_w'], lw['gmlp_b'])
    o_c, v_first, shift_row, wkv = rwkv7_time_mix(zc, shift_prev, wkv_prev, lw, v_first, vres)
    branches = jnp.stack([o_a.reshape(B, L, BRANCH_WIDTH), o_b, o_c], axis=0)
    proj = jnp.einsum('nblc,ncd->nbld', branches, lw['w_branch'])
    gates = jax.nn.sigmoid(zg.reshape(B, L, N_BRANCHES, D_MODEL))
    merged = jnp.einsum('blnd,nbld->bld', gates, proj)
    x = x + merged @ lw['w_out']
    x = x + ffn(rms_norm(x, lw['norm_ffn']))
    ple_gate = jax.nn.sigmoid(rms_norm(x, lw['norm_ple']) @ lw['ple_w_gate'])
    x = x + ple_gate * (pe @ lw['ple_w_proj'])
    return x, v_first, (k_rows, v_rows, vb, shift_row, wkv)


def setup_inputs(seed: int = 0) -> dict:
    key = jax.random.key(seed)
    keys = iter(jax.random.split(key, 48))
    f32 = jnp.float32
    BW = BRANCH_WIDTH
    win_buf = min(MAX_WINDOW, PAST_LEN)

    def nrm(shape, scale=1.0):
        return jax.random.normal(next(keys), shape, f32) * scale

    def unif(shape, lo, hi):
        return jax.random.uniform(next(keys), shape, f32, lo, hi)

    def gain(shape, center=1.0):
        return center + nrm(shape, 0.05)

    return {
        'x_prompt': nrm((BATCH, SEQ, D_MODEL)),
        'x_sample': nrm((DEC_BATCH, DEC_SEQ, D_MODEL)),
        'cache_swa_k': nrm((DEPTH, DEC_BATCH, win_buf, SWA_HEADS, HEAD_DIM)),
        'cache_swa_v': nrm((DEPTH, DEC_BATCH, win_buf, SWA_HEADS, HEAD_DIM)),
        'state_rwkv_shift': nrm((DEPTH, DEC_BATCH, RWKV_COLS)),
        'state_rwkv_wkv': nrm((DEPTH, DEC_BATCH, RWKV_HEADS, HEAD_DIM, HEAD_DIM), 0.5),
        'p_prompt': nrm((DEPTH, BATCH, SEQ, PLE_DIM)),
        'p_sample': nrm((DEPTH, DEC_BATCH, DEC_SEQ, PLE_DIM)),
        'norm_mix': gain((DEPTH, D_MODEL)),
        'norm_ffn': gain((DEPTH, D_MODEL)),
        'norm_ple': gain((DEPTH, D_MODEL)),
        'w_in': nrm((DEPTH, D_MODEL, N_IN), D_MODEL ** -0.5),
        'q_norm': gain((DEPTH, HEAD_DIM)),
        'k_norm': gain((DEPTH, HEAD_DIM)),
        'gmlp_norm': gain((DEPTH, BW)),
        'gmlp_w': nrm((DEPTH, GMLP_GROUPS, GMLP_CHUNK, GMLP_CHUNK), GMLP_CHUNK ** -0.5),
        'gmlp_b': 1.0 + nrm((DEPTH, GMLP_GROUPS, GMLP_CHUNK), 0.1),
        'rwkv_mu': unif((DEPTH, RWKV_COLS), 0.0, 1.0),
        'rwkv_w0': unif((DEPTH, BW), -5.0, 0.0),
        'rwkv_w2': nrm((DEPTH, RWKV_DECAY_LORA, BW), 0.1),
        'rwkv_a0': nrm((DEPTH, BW), 0.5),
        'rwkv_a2': nrm((DEPTH, RWKV_A_LORA, BW), RWKV_A_LORA ** -0.5),
        'rwkv_g2': nrm((DEPTH, RWKV_GATE_LORA, BW), RWKV_GATE_LORA ** -0.5),
        'rwkv_kk': gain((DEPTH, BW), 0.85),
        'rwkv_ka': gain((DEPTH, BW)),
        'rwkv_rk': nrm((DEPTH, RWKV_HEADS, HEAD_DIM), 0.1),
        'rwkv_lnx_w': gain((DEPTH, BW)),
        'rwkv_lnx_b': nrm((DEPTH, BW), 0.02),
        'rwkv_v0': nrm((DEPTH - 1, BW), 0.5),
        'rwkv_v1': nrm((DEPTH - 1, BW, RWKV_V_LORA), BW ** -0.5),
        'rwkv_v2': nrm((DEPTH - 1, RWKV_V_LORA, BW), RWKV_V_LORA ** -0.5),
        'w_branch': nrm((DEPTH, N_BRANCHES, BW, D_MODEL), BW ** -0.5),
        'w_out': nrm((DEPTH, D_MODEL, D_MODEL), D_MODEL ** -0.5),
        'ffn_w_gate': nrm((N_DENSE, D_MODEL, FFN_DIM), D_MODEL ** -0.5),
        'ffn_w_up': nrm((N_DENSE, D_MODEL, FFN_DIM), D_MODEL ** -0.5),
        'ffn_w_down': nrm((N_DENSE, FFN_DIM, D_MODEL), FFN_DIM ** -0.5),
        'moe_router': nrm((N_MOE, D_MODEL, N_EXPERTS), D_MODEL ** -0.5),
        'moe_w_gate': nrm((N_MOE, N_EXPERTS, D_MODEL, FFN_DIM), D_MODEL ** -0.5),
        'moe_w_up': nrm((N_MOE, N_EXPERTS, D_MODEL, FFN_DIM), D_MODEL ** -0.5),
        'moe_w_down': nrm((N_MOE, N_EXPERTS, FFN_DIM, D_MODEL), FFN_DIM ** -0.5),
        'ple_w_proj': nrm((DEPTH, PLE_DIM, D_MODEL), PLE_DIM ** -0.5),
        'ple_w_gate': nrm((DEPTH, D_MODEL, D_MODEL), D_MODEL ** -0.5),
    }


def reference(x_prompt, x_sample, cache_swa_k, cache_swa_v, state_rwkv_shift, state_rwkv_wkv,
              p_prompt, p_sample, norm_mix, norm_ffn, norm_ple, w_in, q_norm, k_norm,
              gmlp_norm, gmlp_w, gmlp_b, rwkv_mu, rwkv_w0, rwkv_w2, rwkv_a0, rwkv_a2, rwkv_g2,
              rwkv_kk, rwkv_ka, rwkv_rk, rwkv_lnx_w, rwkv_lnx_b, rwkv_v0, rwkv_v1, rwkv_v2,
              w_branch, w_out, ffn_w_gate, ffn_w_up, ffn_w_down, moe_router, moe_w_gate,
              moe_w_up, moe_w_down, ple_w_proj, ple_w_gate):
    xp, xs = x_prompt, x_sample
    bp, lp = xp.shape[0], xp.shape[1]
    pos_p = jnp.arange(lp, dtype=jnp.int32)
    pos_s = PAST_LEN + jnp.arange(xs.shape[1], dtype=jnp.int32)
    vf_p = None
    vf_s = None
    kp, vp, shp, wkp = [], [], [], []
    ks, vs, gvs, shs, wks = [], [], [], [], []
    for i in range(DEPTH):
        lw = {'norm_mix': norm_mix[i], 'norm_ffn': norm_ffn[i], 'norm_ple': norm_ple[i],
              'w_in': w_in[i], 'q_norm': q_norm[i], 'k_norm': k_norm[i],
              'gmlp_norm': gmlp_norm[i], 'gmlp_w': gmlp_w[i], 'gmlp_b': gmlp_b[i],
              'rwkv_mu': rwkv_mu[i], 'rwkv_w0': rwkv_w0[i], 'rwkv_w2': rwkv_w2[i],
              'rwkv_a0': rwkv_a0[i], 'rwkv_a2': rwkv_a2[i], 'rwkv_g2': rwkv_g2[i],
              'rwkv_kk': rwkv_kk[i], 'rwkv_ka': rwkv_ka[i], 'rwkv_rk': rwkv_rk[i],
              'rwkv_lnx_w': rwkv_lnx_w[i], 'rwkv_lnx_b': rwkv_lnx_b[i],
              'w_branch': w_branch[i], 'w_out': w_out[i],
              'ple_w_proj': ple_w_proj[i], 'ple_w_gate': ple_w_gate[i]}
        vres = None if i == 0 else (rwkv_v0[i - 1], rwkv_v1[i - 1], rwkv_v2[i - 1])
        j = i // 2
        if i % 2 == 0:
            ffn = functools.partial(swiglu, w_gate=ffn_w_gate[j], w_up=ffn_w_up[j],
                                    w_down=ffn_w_down[j])
        else:
            ffn = functools.partial(moe_swiglu, router=moe_router[j], w_gate=moe_w_gate[j],
                                    w_up=moe_w_up[j], w_down=moe_w_down[j])
        xp, vf_p, st_p = decoder_layer(
            xp, p_prompt[i], pos_p, lw, ffn, vres, vf_p, None,
            jnp.zeros((bp, RWKV_COLS), xp.dtype),
            jnp.zeros((bp, RWKV_HEADS, HEAD_DIM, HEAD_DIM), xp.dtype))
        xs, vf_s, st_s = decoder_layer(
            xs, p_sample[i], pos_s, lw, ffn, vres, vf_s,
            (cache_swa_k[i], cache_swa_v[i]), state_rwkv_shift[i], state_rwkv_wkv[i])
        kp.append(st_p[0]); vp.append(st_p[1]); shp.append(st_p[3]); wkp.append(st_p[4])
        ks.append(st_s[0]); vs.append(st_s[1]); gvs.append(st_s[2]); shs.append(st_s[3]); wks.append(st_s[4])
    new_swa_k_prompt = jnp.stack(kp, axis=0)
    new_swa_v_prompt = jnp.stack(vp, axis=0)
    new_rwkv_shift_prompt = jnp.stack(shp, axis=0)
    new_rwkv_wkv_prompt = jnp.stack(wkp, axis=0)
    new_swa_k_sample = jnp.stack(ks, axis=0)
    new_swa_v_sample = jnp.stack(vs, axis=0)
    new_gmlp_v_sample = jnp.stack(gvs, axis=0)
    new_rwkv_shift_sample = jnp.stack(shs, axis=0)
    new_rwkv_wkv_sample = jnp.stack(wks, axis=0)
    return (xp, xs, new_swa_k_prompt, new_swa_v_prompt, new_rwkv_shift_prompt, new_rwkv_wkv_prompt,
            new_swa_k_sample, new_swa_v_sample, new_gmlp_v_sample, new_rwkv_shift_sample,
            new_rwkv_wkv_sample)
```

```python
import functools

import numpy as np
import jax
import jax.numpy as jnp
from jax import lax
from jax.experimental import pallas as pl
from jax.experimental.pallas import tpu as pltpu

F32 = jnp.float32
BF16 = jnp.bfloat16
HI = lax.Precision.HIGHEST

D_MODEL = 1024
BW = 512
HEAD_DIM = 64
N_HEADS = BW // HEAD_DIM
LANES = 128
DILATIONS = (1, 4, 16)
Q_BLOCK = 128
ROPE_THETA = 10000.0
GMLP_CHUNK = 128
GMLP_GROUPS = 4
RWKV_LORA = (64, 64, 160)
RWKV_LORA_PAD = (128, 128, 256)
RWKV_COLS = 3 * BW + sum(RWKV_LORA)
RWKV_COLS_PAD = 3 * BW + sum(RWKV_LORA_PAD)
RWKV_CHUNK = 64
N_EXPERTS = 8
RMS_EPS = 1e-6
LNX_EPS = 64e-5
NEG_INF = -1e30
VMEM_LIMIT = 48 * 1024 * 1024


def _cparams(n_axes, vmem=VMEM_LIMIT):
    return pltpu.CompilerParams(dimension_semantics=("arbitrary",) * n_axes,
                                vmem_limit_bytes=vmem)


def _rms(x, g):
    return x * lax.rsqrt(jnp.mean(x * x, axis=-1, keepdims=True) + RMS_EPS) * g


def _bdot(a, b):
    return jnp.dot(a.astype(BF16), b.astype(BF16), preferred_element_type=F32)


def _hdot(a, b):
    return jnp.dot(a, b, precision=HI, preferred_element_type=F32)


def _dot_nt(a, b, precision=None):
    return lax.dot_general(a, b, (((1,), (1,)), ((), ())), precision=precision,
                           preferred_element_type=F32)


def _dot_tn(a, b, precision=None):
    return lax.dot_general(a, b, (((0,), (0,)), ((), ())), precision=precision,
                           preferred_element_type=F32)


def _sigmoid(x):
    return 1.0 / (1.0 + jnp.exp(-x))


def _rows_call(body, n_rows, tm, rows, consts, outs, scratch=()):
    in_specs = [pl.BlockSpec((tm, a.shape[1]), lambda i: (i, 0)) for a in rows]
    in_specs += [pl.BlockSpec(a.shape, lambda i, n=a.ndim: (0,) * n) for a in consts]
    out_specs = [pl.BlockSpec((tm, w), lambda i: (i, 0)) for w, _ in outs]
    out_shape = [jax.ShapeDtypeStruct((n_rows, w), dt) for w, dt in outs]
    return pl.pallas_call(body, grid=(n_rows // tm,), in_specs=in_specs, out_specs=out_specs,
                          out_shape=out_shape, scratch_shapes=list(scratch),
                          compiler_params=_cparams(1))(*rows, *consts)


def _head_sumsq(t, ones_bd):
    return _hdot(t * t, ones_bd)


def _attn_inproj_body(x_ref, cos_ref, sin_ref, g_ref, w_ref, qg_ref, kg_ref, ones_ref,
                      q_ref, k_ref, v_ref):
    xn = _rms(x_ref[...], g_ref[...])
    z = _bdot(xn, w_ref[...])
    cos = cos_ref[...]
    sin = sin_ref[...]
    lane = lax.broadcasted_iota(jnp.int32, cos.shape, 1)
    first_half = (lane % HEAD_DIM) < (HEAD_DIM // 2)
    ones_bd = ones_ref[...]
    for off, gain_ref, o_ref in ((0, qg_ref, q_ref), (BW, kg_ref, k_ref)):
        gain = gain_ref[...]
        for j in range(BW // LANES):
            t = z[:, off + LANES * j: off + LANES * (j + 1)]
            tn = t * lax.rsqrt(_head_sumsq(t, ones_bd) * (1.0 / HEAD_DIM) + RMS_EPS) * gain
            swapped = jnp.where(first_half, pltpu.roll(tn, LANES - HEAD_DIM // 2, 1),
                                pltpu.roll(tn, HEAD_DIM // 2, 1))
            o_ref[:, LANES * j: LANES * (j + 1)] = tn * cos + swapped * sin
    v_ref[...] = z[:, 2 * BW:]


def _attn_stream_body(q_ref, kc_ref, kp_ref, vc_ref, vp_ref, o_ref, lse_ref):
    has_prev = pl.program_id(1) > 0
    row = lax.broadcasted_iota(jnp.int32, (Q_BLOCK, Q_BLOCK), 0)
    col = lax.broadcasted_iota(jnp.int32, (Q_BLOCK, Q_BLOCK), 1)
    cur_ok = col <= row
    prev_ok = jnp.logical_and(col >= row, has_prev)
    scale = HEAD_DIM ** -0.5
    for h in range(LANES // HEAD_DIM):
        sl = slice(HEAD_DIM * h, HEAD_DIM * (h + 1))
        q = (q_ref[:, sl] * scale).astype(BF16)
        sc = jnp.where(cur_ok, _dot_nt(q, kc_ref[:, sl].astype(BF16)), NEG_INF)
        sp = jnp.where(prev_ok, _dot_nt(q, kp_ref[:, sl].astype(BF16)), NEG_INF)
        m = jnp.maximum(jnp.max(sc, axis=-1, keepdims=True), jnp.max(sp, axis=-1, keepdims=True))
        pc = jnp.exp(sc - m)
        pp = jnp.exp(sp - m)
        den = jnp.sum(pc, axis=-1, keepdims=True) + jnp.sum(pp, axis=-1, keepdims=True)
        o = _bdot(pc, vc_ref[:, sl]) + _bdot(pp, vp_ref[:, sl])
        o_ref[:, sl] = o / den
        lse_ref[:, sl] = jnp.broadcast_to(m + jnp.log(den), (Q_BLOCK, HEAD_DIM))


def _attn_stream(q, k, v, dil):
    b, s, _ = q.shape
    nblk = s // (Q_BLOCK * dil)
    cols = dil * BW
    view = lambda t: t.reshape(b, nblk, Q_BLOCK, cols)
    n_cb = cols // LANES
    blk = (None, None, Q_BLOCK, LANES)
    cur = pl.BlockSpec(blk, lambda bi, m, c: (bi, m, 0, c))
    prev = pl.BlockSpec(blk, lambda bi, m, c: (bi, jnp.maximum(m - 1, 0), 0, c))
    o, lse = pl.pallas_call(
        _attn_stream_body, grid=(b, nblk, n_cb),
        in_specs=[cur, cur, prev, cur, prev], out_specs=[cur, cur],
        out_shape=[jax.ShapeDtypeStruct((b, nblk, Q_BLOCK, cols), F32)] * 2,
        compiler_params=_cparams(3))(view(q), view(k), view(k), view(v), view(v))
    return o.reshape(b * s, BW), lse.reshape(b * s, BW)


def _attn_combine_body(o1_ref, o2_ref, o3_ref, l1_ref, l2_ref, l3_ref, out_ref):
    l1, l2, l3 = l1_ref[...], l2_ref[...], l3_ref[...]
    m = jnp.maximum(jnp.maximum(l1, l2), l3)
    e1, e2, e3 = jnp.exp(l1 - m), jnp.exp(l2 - m), jnp.exp(l3 - m)
    out_ref[...] = (e1 * o1_ref[...] + e2 * o2_ref[...] + e3 * o3_ref[...]) / (e1 + e2 + e3)


def _attn_sample_body(q_ref, kn_ref, vn_ref, k1_ref, v1_ref, k4_ref, v4_ref, k16_ref, v16_ref,
                      e8_ref, o_ref, *, n_new):
    e8 = e8_ref[...]
    scale = HEAD_DIM ** -0.5
    kn = kn_ref[...]
    vn = vn_ref[...]
    crow = lax.broadcasted_iota(jnp.int32, (Q_BLOCK, N_HEADS), 0)
    nrow = lax.broadcasted_iota(jnp.int32, (n_new, N_HEADS), 0)
    n_pat = float(len(DILATIONS))
    for t in range(n_new):
        qbd = (e8 * (q_ref[t:t + 1, :] * scale)).astype(BF16)
        sl = slice(BW * t, BW * (t + 1))
        s1 = jnp.where(crow >= t, _dot_nt(k1_ref[...].astype(BF16), qbd), NEG_INF)
        s4 = _dot_nt(k4_ref[:, sl].astype(BF16), qbd)
        s16 = _dot_nt(k16_ref[:, sl].astype(BF16), qbd)
        sn = jnp.where(nrow <= t, _dot_nt(kn.astype(BF16), qbd), NEG_INF)
        cnt = jnp.where(nrow == t, n_pat, 1.0)
        m = jnp.max(sn, axis=0, keepdims=True)
        for s in (s1, s4, s16):
            m = jnp.maximum(m, jnp.max(s, axis=0, keepdims=True))
        pn = cnt * jnp.exp(sn - m)
        den = jnp.sum(pn, axis=0, keepdims=True)
        acc = jnp.sum(_bdot(pn, e8) * vn, axis=0, keepdims=True)
        for s, vv in ((s1, v1_ref[...]), (s4, v4_ref[:, sl]), (s16, v16_ref[:, sl])):
            p = jnp.exp(s - m)
            den = den + jnp.sum(p, axis=0, keepdims=True)
            acc = acc + jnp.sum(_bdot(p, e8) * vv, axis=0, keepdims=True)
        o_ref[t:t + 1, :] = acc / _hdot(den, e8)


def _attn_sample(q, kn, vn, cache_k, cache_v, layer, e8):
    b, n_new, _ = q.shape
    lc = cache_k.shape[2]
    assert lc == Q_BLOCK * DILATIONS[-1] and n_new <= DILATIONS[1]
    new_spec = pl.BlockSpec((None, n_new, BW), lambda i: (i, 0, 0))
    specs, args = [new_spec] * 3, [q, kn, vn]
    for dil in DILATIONS:
        rows, cols = lc // dil, dil * BW
        width = min(cols, n_new * BW)
        last = rows // Q_BLOCK - 1
        spec = pl.BlockSpec((None, None, Q_BLOCK, width),
                            lambda i, last=last: (layer, i, last, 0))
        specs += [spec, spec]
        args += [cache_k.reshape(-1, b, rows, cols), cache_v.reshape(-1, b, rows, cols)]
    specs.append(pl.BlockSpec(e8.shape, lambda i: (0, 0)))
    args.append(e8)
    return pl.pallas_call(
        functools.partial(_attn_sample_body, n_new=n_new), grid=(b,),
        in_specs=specs, out_specs=new_spec,
        out_shape=jax.ShapeDtypeStruct((b, n_new, BW), F32),
        compiler_params=_cparams(1))(*args)


def _gelu_tanh(x):
    return 0.5 * x * (1.0 + jnp.tanh(0.7978845608028654 * (x + 0.044715 * (x * x * x))))


def _gmlp_body(x_ref, g_ref, w_ref, gn_ref, wm_ref, bias_ref, ob_ref, vb_ref):
    xn = _rms(x_ref[...], g_ref[...])
    z = _gelu_tanh(_bdot(xn, w_ref[...]))
    u = z[:, :BW]
    vb = _rms(z[:, BW:], gn_ref[...])
    vb_ref[...] = vb
    bias = bias_ref[...]
    gw = BW // GMLP_GROUPS
    for c in range(x_ref.shape[0] // GMLP_CHUNK):
        rows = slice(GMLP_CHUNK * c, GMLP_CHUNK * (c + 1))
        f = jnp.concatenate([_bdot(wm_ref[g], vb[rows, gw * g: gw * (g + 1)])
                             for g in range(GMLP_GROUPS)], axis=1)
        ob_ref[rows, :] = u[rows, :] * (f + bias)


def _rwkv_prep_body(*refs, seq_len, has_vres):
    refs = list(refs)
    x_ref, init_ref = refs[:2]
    pos = 2
    vf_ref = None
    if has_vres:
        vf_ref = refs[pos]
        pos += 1
    (g_ref, w_ref, mu_ref, w0_ref, w2_ref, a0_ref, a2_ref, g2_ref, kkw_ref, ka_ref,
     ones_ref) = refs[pos:pos + 11]
    pos += 11
    if has_vres:
        v0_ref, v1_ref, v2_ref = refs[pos:pos + 3]
        pos += 3
    z_ref, r_ref, lw_ref, k_ref, v_ref, kk_ref, a_ref, gate_ref = refs[pos:pos + 8]
    carry_ref = refs[pos + 8]

    tm = x_ref.shape[0]
    z = _bdot(_rms(x_ref[...], g_ref[...]), w_ref[...])
    z_ref[...] = z
    rolled = pltpu.roll(z, 1, 0)
    row = lax.broadcasted_iota(jnp.int32, (tm, 1), 0)
    if seq_len <= tm:
        zprev = jnp.where(row % seq_len == 0, init_ref[...], rolled)
    else:
        i = pl.program_id(0)

        @pl.when(i % (seq_len // tm) == 0)
        def _():
            carry_ref[...] = jnp.zeros_like(carry_ref)

        zprev = jnp.where(row == 0, carry_ref[...], rolled)
        carry_ref[...] = z[tm - 1:tm, :]
    zs = z + (zprev - z) * mu_ref[...]
    o_w = 3 * BW
    o_a = o_w + RWKV_LORA_PAD[0]
    o_g = o_a + RWKV_LORA_PAD[1]
    r, k, v = zs[:, :BW], zs[:, BW:2 * BW], zs[:, 2 * BW:3 * BW]
    wd, ad, gd = zs[:, o_w:o_a], zs[:, o_a:o_g], zs[:, o_g:]
    t = -(w0_ref[...] + _hdot(jnp.tanh(wd), w2_ref[...]))
    softplus = jnp.maximum(t, 0.0) + jnp.log(1.0 + jnp.exp(-jnp.abs(t)))
    lw_ref[...] = -jnp.exp(-softplus - 0.5)
    a = _sigmoid(a0_ref[...] + _hdot(ad, a2_ref[...]))
    gate_ref[...] = _hdot(_sigmoid(gd), g2_ref[...])
    if has_vres:
        mix = _sigmoid(v0_ref[...] + _hdot(_hdot(v, v1_ref[...]), v2_ref[...]))
        v = v + (vf_ref[...] - v) * mix
    kk = k * kkw_ref[...]
    ones_bd = ones_ref[...]
    for j in range(BW // LANES):
        sl = slice(LANES * j, LANES * (j + 1))
        t = kk[:, sl]
        kk_ref[:, sl] = t / jnp.maximum(jnp.sqrt(_head_sumsq(t, ones_bd)), 1e-12)
    r_ref[...] = r
    k_ref[...] = k * (1.0 + (a - 1.0) * ka_ref[...])
    v_ref[...] = v
    a_ref[...] = a


def _rwkv_post(y, r, k, v, gate, rk, lnw, lnb):
    mu = jnp.mean(y, axis=-1, keepdims=True)
    d = y - mu
    var = jnp.mean(d * d, axis=-1, keepdims=True)
    yn = d * lax.rsqrt(var + LNX_EPS) * lnw + lnb
    bonus = jnp.sum(r * k * rk, axis=-1, keepdims=True) * v
    return (yn + bonus) * gate


def _rwkv_chunk_body(r_ref, lw_ref, k_ref, v_ref, kk_ref, a_ref, gate_ref, rk_ref, lnw_ref, lnb_ref,
                     tri_ref, o_ref, sfin_ref, s_ref):
    C = RWKV_CHUNK
    j = pl.program_id(1)

    @pl.when(j == 0)
    def _():
        s_ref[...] = jnp.zeros_like(s_ref)

    rowi = lax.broadcasted_iota(jnp.int32, (C, C), 0)
    coli = lax.broadcasted_iota(jnp.int32, (C, C), 1)
    strict = coli < rowi
    incl = coli <= rowi
    eye = (coli == rowi).astype(F32)
    tri = tri_ref[...]
    rk, lnw, lnb = rk_ref[...], lnw_ref[...], lnb_ref[...]

    def chunk(c, carry):
        rows = pl.ds(pl.multiple_of(c * C, C), C)
        lw = lw_ref[rows, :]
        r, k, v, kk, a = r_ref[rows, :], k_ref[rows, :], v_ref[rows, :], kk_ref[rows, :], a_ref[rows, :]
        gate = gate_ref[rows, :]
        cs = _hdot(tri, lw)
        e_inv = jnp.exp(-cs)
        a_t = -kk * jnp.exp(cs - lw)
        b_t = kk * a * e_inv
        k_t = k * e_inv
        r_t = r * jnp.exp(cs)
        gam = jnp.exp(cs[C - 1:C, :])
        for h in range(N_HEADS):
            sl = slice(HEAD_DIM * h, HEAD_DIM * (h + 1))
            ah, bh, kh, rh, vh = a_t[:, sl], b_t[:, sl], k_t[:, sl], r_t[:, sl], v[:, sl]
            g = _dot_nt(jnp.concatenate([ah, rh], axis=0).astype(BF16),
                        jnp.concatenate([bh, kh], axis=0).astype(BF16))
            l_ab = jnp.where(strict, g[:C, :C], 0.0)
            l_ak = jnp.where(strict, g[:C, C:], 0.0)
            m_rb = jnp.where(incl, g[C:, :C], 0.0)
            m_rk = jnp.where(incl, g[C:, C:], 0.0)
            tinv = eye + l_ab
            p = l_ab
            for _ in range(int(np.log2(C)) - 1):
                p = _hdot(p, p)
                tinv = tinv + _hdot(tinv, p)
            w1 = _bdot(l_ak, vh)
            av = _hdot(tinv, jnp.concatenate([ah, w1], axis=1))
            ry = _bdot(m_rb, av)
            r_hat = rh + ry[:, :HEAD_DIM]
            y_hat = ry[:, HEAD_DIM:] + _bdot(m_rk, vh)
            pq = _dot_tn(av.astype(BF16), bh.astype(BF16))
            q_t = pq[HEAD_DIM:, :] + _dot_tn(vh.astype(BF16), kh.astype(BF16))
            s = s_ref[h]
            y = _dot_nt(r_hat.astype(BF16), s.astype(BF16)) + y_hat
            s_ref[h] = (s + _bdot(s, pq[:HEAD_DIM, :]) + q_t) * gam[:, sl]
            o_ref[rows, sl] = _rwkv_post(y, r[:, sl], k[:, sl], vh, gate[:, sl],
                                         rk[:, sl], lnw[:, sl], lnb[:, sl])
        return carry

    lax.fori_loop(0, r_ref.shape[0] // C, chunk, 0)

    @pl.when(j == pl.num_programs(1) - 1)
    def _():
        sfin_ref[...] = s_ref[...]


def _rwkv_chunk_scan(seqs, gate, rk, lnw, lnb, batch, seq_len, tc=512):
    n = batch * seq_len
    tri = jnp.tril(jnp.ones((RWKV_CHUNK, RWKV_CHUNK), F32))
    steps = seq_len // tc
    row_spec = pl.BlockSpec((tc, BW), lambda b, j: (b * steps + j, 0))
    vec_spec = pl.BlockSpec((1, BW), lambda b, j: (0, 0))
    state_spec = pl.BlockSpec((None, N_HEADS, HEAD_DIM, HEAD_DIM), lambda b, j: (b, 0, 0, 0))
    return pl.pallas_call(
        _rwkv_chunk_body, grid=(batch, steps),
        in_specs=[row_spec] * 7 + [vec_spec] * 3 + [pl.BlockSpec(tri.shape, lambda b, j: (0, 0))],
        out_specs=[row_spec, state_spec],
        out_shape=[jax.ShapeDtypeStruct((n, BW), F32),
                   jax.ShapeDtypeStruct((batch, N_HEADS, HEAD_DIM, HEAD_DIM), F32)],
        scratch_shapes=[pltpu.VMEM((N_HEADS, HEAD_DIM, HEAD_DIM), F32)],
        compiler_params=_cparams(2))(*seqs, gate, rk, lnw, lnb, tri)


def _rwkv_steps_body(r_ref, lw_ref, k_ref, v_ref, kk_ref, a_ref, gate_ref, rk_ref, lnw_ref, lnb_ref,
                     s0_ref, o_ref, sfin_ref, *, seq_len):
    n_seq = r_ref.shape[0] // seq_len
    rowi = lax.broadcasted_iota(jnp.int32, (HEAD_DIM, HEAD_DIM), 0)
    coli = lax.broadcasted_iota(jnp.int32, (HEAD_DIM, HEAD_DIM), 1)
    eye = (coli == rowi).astype(F32)
    for b in range(n_seq):
        for h in range(N_HEADS):
            sl = slice(HEAD_DIM * h, HEAD_DIM * (h + 1))
            s = s0_ref[b, h]
            for t in range(seq_len):
                i = b * seq_len + t
                row = lambda ref: ref[i:i + 1, sl]
                r, k, v, kk, a = row(r_ref), row(k_ref), row(v_ref), row(kk_ref), row(a_ref)
                w = jnp.exp(row(lw_ref))
                sa = -jnp.sum(s * kk, axis=-1, keepdims=True)
                v_col = jnp.sum(eye * v, axis=-1, keepdims=True)
                s = s * w + sa * (kk * a) + v_col * k
                y_col = jnp.sum(s * r, axis=-1, keepdims=True)
                y = jnp.sum(eye * y_col, axis=0, keepdims=True)
                o_ref[i:i + 1, sl] = _rwkv_post(y, r, k, v, row(gate_ref), rk_ref[:, sl],
                                                lnw_ref[:, sl], lnb_ref[:, sl])
            sfin_ref[b, h] = s


def _rwkv_steps_scan(seqs, gate, rk, lnw, lnb, s0, batch, seq_len):
    n = batch * seq_len
    per = 8 // seq_len if seq_len < 8 else 1
    assert (per * seq_len) % 8 == 0 and batch % per == 0
    tm = per * seq_len
    row_spec = pl.BlockSpec((tm, BW), lambda i: (i, 0))
    vec_spec = pl.BlockSpec((1, BW), lambda i: (0, 0))
    state_spec = pl.BlockSpec((per, N_HEADS, HEAD_DIM, HEAD_DIM), lambda i: (i, 0, 0, 0))
    return pl.pallas_call(
        functools.partial(_rwkv_steps_body, seq_len=seq_len), grid=(batch // per,),
        in_specs=[row_spec] * 7 + [vec_spec] * 3 + [state_spec],
        out_specs=[row_spec, state_spec],
        out_shape=[jax.ShapeDtypeStruct((n, BW), F32),
                   jax.ShapeDtypeStruct((batch, N_HEADS, HEAD_DIM, HEAD_DIM), F32)],
        compiler_params=_cparams(1))(*seqs, gate, rk, lnw, lnb, s0)


def _merge_body(x_ref, oa_ref, ob_ref, oc_ref, g_ref, wg_ref, wb_ref, wo_ref, out_ref):
    x = x_ref[...]
    xn = _rms(x, g_ref[...]).astype(BF16)
    merged = None
    for n, o_ref in enumerate((oa_ref, ob_ref, oc_ref)):
        gate = _sigmoid(jnp.dot(xn, wg_ref[:, D_MODEL * n: D_MODEL * (n + 1)],
                                preferred_element_type=F32))
        term = gate * _bdot(o_ref[...], wb_ref[n])
        merged = term if merged is None else merged + term
    out_ref[...] = x + _bdot(merged, wo_ref[...])


def _silu(x):
    return x * _sigmoid(x)


def _ffn_body(x_ref, g_ref, wg_ref, wu_ref, wd_ref, out_ref, h_ref, acc_ref):
    f = pl.program_id(1)

    @pl.when(f == 0)
    def _():
        h_ref[...] = _rms(x_ref[...], g_ref[...]).astype(BF16)
        acc_ref[...] = jnp.zeros_like(acc_ref)

    h = h_ref[...]
    act = _silu(jnp.dot(h, wg_ref[...], preferred_element_type=F32)) * jnp.dot(
        h, wu_ref[...], preferred_element_type=F32)
    acc_ref[...] += _bdot(act, wd_ref[...])

    @pl.when(f == pl.num_programs(1) - 1)
    def _():
        out_ref[...] = x_ref[...] + acc_ref[...]


def _ffn_dense(x, gain, wg, wu, wd, tm, tf):
    n, d = x.shape
    ffn = wg.shape[1]
    row = pl.BlockSpec((tm, d), lambda i, f: (i, 0))
    return pl.pallas_call(
        _ffn_body, grid=(n // tm, ffn // tf),
        in_specs=[row, pl.BlockSpec((1, d), lambda i, f: (0, 0)),
                  pl.BlockSpec((d, tf), lambda i, f: (0, f)),
                  pl.BlockSpec((d, tf), lambda i, f: (0, f)),
                  pl.BlockSpec((tf, d), lambda i, f: (f, 0))],
        out_specs=row, out_shape=jax.ShapeDtypeStruct((n, d), F32),
        scratch_shapes=[pltpu.VMEM((tm, d), BF16), pltpu.VMEM((tm, d), F32)],
        compiler_params=_cparams(2))(x, gain, wg, wu, wd)


def _moe_body(x_ref, g_ref, rt_ref, wg_ref, wu_ref, wd_ref, out_ref, h_ref, comb_ref, acc_ref):
    e = pl.program_id(1)
    f = pl.program_id(2)
    tm = x_ref.shape[0]
    lane = lax.broadcasted_iota(jnp.int32, (tm, LANES), 1)

    @pl.when(jnp.logical_and(e == 0, f == 0))
    def _():
        h = _rms(x_ref[...], g_ref[...])
        h_ref[...] = h.astype(BF16)
        logits = jnp.where(lane < N_EXPERTS, _hdot(h, rt_ref[...]), -jnp.inf)
        m1 = jnp.max(logits, axis=-1, keepdims=True)
        i1 = jnp.min(jnp.where(logits == m1, lane, LANES), axis=-1, keepdims=True)
        rest = jnp.where(lane == i1, -jnp.inf, logits)
        m2 = jnp.max(rest, axis=-1, keepdims=True)
        i2 = jnp.min(jnp.where(rest == m2, lane, LANES), axis=-1, keepdims=True)
        e2 = jnp.exp(m2 - m1)
        comb_ref[...] = (jnp.where(lane == i1, 1.0 / (1.0 + e2), 0.0)
                         + jnp.where(lane == i2, e2 / (1.0 + e2), 0.0))
        acc_ref[...] = jnp.zeros_like(acc_ref)

    h = h_ref[...]
    act = _silu(jnp.dot(h, wg_ref[...], preferred_element_type=F32)) * jnp.dot(
        h, wu_ref[...], preferred_element_type=F32)
    weight = jnp.sum(jnp.where(lane == e, comb_ref[...], 0.0), axis=-1, keepdims=True)
    acc_ref[...] += _bdot(act * weight, wd_ref[...])

    @pl.when(jnp.logical_and(e == pl.num_programs(1) - 1, f == pl.num_programs(2) - 1))
    def _():
        out_ref[...] = x_ref[...] + acc_ref[...]


def _ffn_moe(x, gain, router, wg, wu, wd, tm, tf):
    n, d = x.shape
    n_exp, _, ffn = wg.shape
    row = pl.BlockSpec((tm, d), lambda i, e, f: (i, 0))
    return pl.pallas_call(
        _moe_body, grid=(n // tm, n_exp, ffn // tf),
        in_specs=[row, pl.BlockSpec((1, d), lambda i, e, f: (0, 0)),
                  pl.BlockSpec(router.shape, lambda i, e, f: (0, 0)),
                  pl.BlockSpec((None, d, tf), lambda i, e, f: (e, 0, f)),
                  pl.BlockSpec((None, d, tf), lambda i, e, f: (e, 0, f)),
                  pl.BlockSpec((None, tf, d), lambda i, e, f: (e, f, 0))],
        out_specs=row, out_shape=jax.ShapeDtypeStruct((n, d), F32),
        scratch_shapes=[pltpu.VMEM((tm, d), BF16), pltpu.VMEM((tm, LANES), F32),
                        pltpu.VMEM((tm, d), F32)],
        compiler_params=_cparams(3))(x, gain, router, wg, wu, wd)


def _ple_body(x_ref, pe_ref, g_ref, wgate_ref, wproj_ref, out_ref):
    x = x_ref[...]
    gate = _sigmoid(_bdot(_rms(x, g_ref[...]), wgate_ref[...]))
    out_ref[...] = x + gate * _bdot(pe_ref[...], wproj_ref[...])


def _pad_rwkv_cols(t):
    parts = [t[..., :3 * BW]]
    off = 3 * BW
    for width, padded in zip(RWKV_LORA, RWKV_LORA_PAD):
        seg = t[..., off:off + width]
        parts.append(jnp.pad(seg, [(0, 0)] * (t.ndim - 1) + [(0, padded - width)]))
        off += width
    return jnp.concatenate(parts, axis=-1)


def _unpad_rwkv_cols(t):
    parts = [t[..., :3 * BW]]
    off = 3 * BW
    for width, padded in zip(RWKV_LORA, RWKV_LORA_PAD):
        parts.append(t[..., off:off + width])
        off += padded
    return jnp.concatenate(parts, axis=-1)


def _pad_rows(w, rows):
    return jnp.pad(w, ((0, rows - w.shape[0]), (0, 0)))


def _rope_tables(pos):
    half = HEAD_DIM // 2
    inv = 1.0 / (ROPE_THETA ** (jnp.arange(half, dtype=F32) / half))
    ang = pos.astype(F32)[:, None] * inv[None, :]
    cos, sin = jnp.cos(ang), jnp.sin(ang)
    reps = LANES // HEAD_DIM
    return (jnp.tile(jnp.concatenate([cos, cos], axis=-1), (1, reps)),
            jnp.tile(jnp.concatenate([-sin, sin], axis=-1), (1, reps)))


def _layer_weights(i, p):
    row = lambda t: t.reshape(1, -1)
    w_in = p['w_in'][i]
    o_b, o_c, o_g = 3 * BW, 5 * BW, 5 * BW + RWKV_COLS
    tril = jnp.tril(jnp.ones((GMLP_CHUNK, GMLP_CHUNK), F32))
    gw = p['gmlp_w'][i]
    gb = p['gmlp_b'][i]
    lw = {
        'norm_mix': row(p['norm_mix'][i]), 'norm_ffn': row(p['norm_ffn'][i]),
        'norm_ple': row(p['norm_ple'][i]),
        'w_a': w_in[:, :o_b].astype(BF16), 'w_b': w_in[:, o_b:o_c].astype(BF16),
        'w_c': _pad_rwkv_cols(w_in[:, o_c:o_g]).astype(BF16), 'w_g': w_in[:, o_g:].astype(BF16),
        'q_norm': row(jnp.tile(p['q_norm'][i], LANES // HEAD_DIM)),
        'k_norm': row(jnp.tile(p['k_norm'][i], LANES // HEAD_DIM)),
        'gmlp_norm': row(p['gmlp_norm'][i]),
        'gmlp_w_full': (gw * tril).astype(BF16),
        'gmlp_b_full': jnp.repeat(gb.T, BW // GMLP_GROUPS, axis=1),
        'gmlp_w': gw, 'gmlp_b': gb,
        'mu': row(_pad_rwkv_cols(p['rwkv_mu'][i])),
        'w0': row(p['rwkv_w0'][i]), 'w2': _pad_rows(p['rwkv_w2'][i], RWKV_LORA_PAD[0]),
        'a0': row(p['rwkv_a0'][i]), 'a2': _pad_rows(p['rwkv_a2'][i], RWKV_LORA_PAD[1]),
        'g2': _pad_rows(p['rwkv_g2'][i], RWKV_LORA_PAD[2]),
        'kk': row(p['rwkv_kk'][i]), 'ka': row(p['rwkv_ka'][i]), 'rk': row(p['rwkv_rk'][i]),
        'lnx_w': row(p['rwkv_lnx_w'][i]), 'lnx_b': row(p['rwkv_lnx_b'][i]),
        'w_branch': p['w_branch'][i].astype(BF16), 'w_out': p['w_out'][i].astype(BF16),
        'ple_w_proj': p['ple_w_proj'][i].astype(BF16), 'ple_w_gate': p['ple_w_gate'][i].astype(BF16),
    }
    if i > 0:
        lw['vres'] = (row(p['rwkv_v0'][i - 1]), p['rwkv_v1'][i - 1], p['rwkv_v2'][i - 1])
    if i % 2 == 0:
        j = i // 2
        lw['ffn'] = tuple(p[n][j].astype(BF16) for n in ('ffn_w_gate', 'ffn_w_up', 'ffn_w_down'))
    else:
        j = i // 2
        router = jnp.pad(p['moe_router'][j], ((0, 0), (0, LANES - N_EXPERTS)))
        lw['moe'] = (router,) + tuple(p[n][j].astype(BF16)
                                      for n in ('moe_w_gate', 'moe_w_up', 'moe_w_down'))
    return lw


def _gmlp_mix_tables(lw, seq_len):
    if seq_len >= GMLP_CHUNK:
        return lw['gmlp_w_full'], lw['gmlp_b_full']
    lc = seq_len
    tril = jnp.tril(jnp.ones((lc, lc), F32))
    small = lw['gmlp_w'][:, :lc, :lc] * tril
    reps = GMLP_CHUNK // lc
    eye = jnp.eye(reps, dtype=F32)
    wm = jnp.einsum('ab,gij->gaibj', eye, small).reshape(GMLP_GROUPS, GMLP_CHUNK, GMLP_CHUNK)
    bias = jnp.tile(jnp.repeat(lw['gmlp_b'][:, :lc].T, BW // GMLP_GROUPS, axis=1), (reps, 1))
    return wm.astype(BF16), bias


def _decoder_layer(i, x, pe, pos, lw, consts, batch, seq_len, tm, v_first, cache, shift_prev,
                   wkv_prev):
    n = x.shape[0]
    ones_bd, e8 = consts
    cos, sin = _rope_tables(pos)
    q, k, v = _rows_call(_attn_inproj_body, n, tm, [x, cos, sin],
                         [lw['norm_mix'], lw['w_a'], lw['q_norm'], lw['k_norm'], ones_bd],
                         [(BW, F32)] * 3)
    if cache is None:
        seq = lambda t: t.reshape(batch, seq_len, BW)
        parts = [_attn_stream(seq(q), seq(k), seq(v), dil) for dil in DILATIONS]
        (o_a,) = _rows_call(_attn_combine_body, n, tm, [pt[0] for pt in parts] + [pt[1] for pt in parts],
                            [], [(BW, F32)])
    else:
        seq = lambda t: t.reshape(batch, seq_len, BW)
        o_a = _attn_sample(seq(q), seq(k), seq(v), cache[0], cache[1], i, e8).reshape(n, BW)
    wm, bias = _gmlp_mix_tables(lw, seq_len)
    o_b, vb = _rows_call(_gmlp_body, n, tm, [x],
                         [lw['norm_mix'], lw['w_b'], lw['gmlp_norm'], wm, bias], [(BW, F32)] * 2)
    has_vres = 'vres' in lw
    if seq_len <= tm:
        init = jnp.zeros((batch, seq_len, RWKV_COLS_PAD), F32).at[:, 0].set(
            _pad_rwkv_cols(shift_prev)).reshape(n, RWKV_COLS_PAD)
    else:
        init = jnp.zeros((n, LANES), F32)
    rows = [x, init] + ([v_first] if has_vres else [])
    cst = [lw['norm_mix'], lw['w_c'], lw['mu'], lw['w0'], lw['w2'], lw['a0'], lw['a2'], lw['g2'],
           lw['kk'], lw['ka'], ones_bd] + (list(lw['vres']) if has_vres else [])
    z, r, lwd, k_c, v_c, kk, a, gate = _rows_call(
        functools.partial(_rwkv_prep_body, seq_len=seq_len, has_vres=has_vres), n, tm, rows, cst,
        [(RWKV_COLS_PAD, F32)] + [(BW, F32)] * 7,
        scratch=[pltpu.VMEM((1, RWKV_COLS_PAD), F32)])
    if v_first is None:
        v_first = v_c
    seqs = (r, lwd, k_c, v_c, kk, a)
    if wkv_prev is None:
        o_c, wkv = _rwkv_chunk_scan(seqs, gate, lw['rk'], lw['lnx_w'], lw['lnx_b'], batch, seq_len)
    else:
        o_c, wkv = _rwkv_steps_scan(seqs, gate, lw['rk'], lw['lnx_w'], lw['lnx_b'], wkv_prev,
                                    batch, seq_len)
    shift_row = _unpad_rwkv_cols(z.reshape(batch, seq_len, RWKV_COLS_PAD)[:, -1])
    (x,) = _rows_call(_merge_body, n, tm, [x, o_a, o_b, o_c],
                      [lw['norm_mix'], lw['w_g'], lw['w_branch'], lw['w_out']], [(D_MODEL, F32)])
    if 'ffn' in lw:
        x = _ffn_dense(x, lw['norm_ffn'], *lw['ffn'], tm=tm, tf=256)
    else:
        x = _ffn_moe(x, lw['norm_ffn'], *lw['moe'], tm=tm, tf=256)
    (x,) = _rows_call(_ple_body, n, tm, [x, pe], [lw['norm_ple'], lw['ple_w_gate'], lw['ple_w_proj']],
                      [(D_MODEL, F32)])
    return x, v_first, (k, v, vb, shift_row, wkv)


def kernel(x_prompt, x_sample, cache_swa_k, cache_swa_v, state_rwkv_shift, state_rwkv_wkv, p_prompt, p_sample, norm_mix, norm_ffn, norm_ple, w_in, q_norm, k_norm, gmlp_norm, gmlp_w, gmlp_b, rwkv_mu, rwkv_w0, rwkv_w2, rwkv_a0, rwkv_a2, rwkv_g2, rwkv_kk, rwkv_ka, rwkv_rk, rwkv_lnx_w, rwkv_lnx_b, rwkv_v0, rwkv_v1, rwkv_v2, w_branch, w_out, ffn_w_gate, ffn_w_up, ffn_w_down, moe_router, moe_w_gate, moe_w_up, moe_w_down, ple_w_proj, ple_w_gate):
    params = dict(norm_mix=norm_mix, norm_ffn=norm_ffn, norm_ple=norm_ple, w_in=w_in, q_norm=q_norm,
                  k_norm=k_norm, gmlp_norm=gmlp_norm, gmlp_w=gmlp_w, gmlp_b=gmlp_b, rwkv_mu=rwkv_mu,
                  rwkv_w0=rwkv_w0, rwkv_w2=rwkv_w2, rwkv_a0=rwkv_a0, rwkv_a2=rwkv_a2,
                  rwkv_g2=rwkv_g2, rwkv_kk=rwkv_kk, rwkv_ka=rwkv_ka, rwkv_rk=rwkv_rk,
                  rwkv_lnx_w=rwkv_lnx_w, rwkv_lnx_b=rwkv_lnx_b, rwkv_v0=rwkv_v0, rwkv_v1=rwkv_v1,
                  rwkv_v2=rwkv_v2, w_branch=w_branch, w_out=w_out, ffn_w_gate=ffn_w_gate,
                  ffn_w_up=ffn_w_up, ffn_w_down=ffn_w_down, moe_router=moe_router,
                  moe_w_gate=moe_w_gate, moe_w_up=moe_w_up, moe_w_down=moe_w_down,
                  ple_w_proj=ple_w_proj, ple_w_gate=ple_w_gate)
    depth = w_in.shape[0]
    bp, lp, d = x_prompt.shape
    bs, ls, _ = x_sample.shape
    past_len = cache_swa_k.shape[2]
    n_p, n_s = bp * lp, bs * ls
    tm_p = 512
    tm_s = n_s

    head_of_lane = np.arange(LANES) // HEAD_DIM
    ones_bd = jnp.asarray(head_of_lane[:, None] == head_of_lane[None, :], F32)
    e8 = jnp.asarray(np.arange(N_HEADS)[:, None] == (np.arange(BW) // HEAD_DIM)[None, :], F32)
    consts = (ones_bd, e8)

    pos_p = jnp.tile(jnp.arange(lp, dtype=jnp.int32), bp)
    pos_s = jnp.tile(past_len + jnp.arange(ls, dtype=jnp.int32), bs)
    xp = x_prompt.reshape(n_p, d)
    xs = x_sample.reshape(n_s, d)
    vf_p = vf_s = None
    st_p, st_s = [], []
    for i in range(depth):
        lw = _layer_weights(i, params)
        xp, vf_p, sp = _decoder_layer(i, xp, p_prompt[i].reshape(n_p, -1), pos_p, lw, consts, bp, lp,
                                      tm_p, vf_p, None, None, None)
        xs, vf_s, ss = _decoder_layer(i, xs, p_sample[i].reshape(n_s, -1), pos_s, lw, consts, bs, ls,
                                      tm_s, vf_s, (cache_swa_k, cache_swa_v), state_rwkv_shift[i],
                                      state_rwkv_wkv[i])
        st_p.append(sp)
        st_s.append(ss)

    keep = min(Q_BLOCK * DILATIONS[-1], lp)
    heads = lambda t, b, l: t.reshape(b, l, N_HEADS, HEAD_DIM)
    stack = lambda items: jnp.stack(items, axis=0)
    return (xp.reshape(bp, lp, d), xs.reshape(bs, ls, d),
            stack([heads(s[0], bp, lp)[:, lp - keep:] for s in st_p]),
            stack([heads(s[1], bp, lp)[:, lp - keep:] for s in st_p]),
            stack([s[3] for s in st_p]),
            stack([s[4] for s in st_p]),
            stack([heads(s[0], bs, ls) for s in st_s]),
            stack([heads(s[1], bs, ls) for s in st_s]),
            stack([s[2].reshape(bs, ls, BW) for s in st_s]),
            stack([s[3] for s in st_s]),
            stack([s[4] for s in st_s]))
```
